```python
import jax, jax.numpy as jnp
from jax import lax
import numpy as np

D_MODEL = 1024
BATCH = 32
SEQ = 2048
DEPTH = 4

GRID_W = 64
CTX_LEN = 256
HEAD_DIM = 64
NA_HEADS = 6
NA_WIN_ROWS = 8
NA_WIN_COLS = 16
GQA_Q_HEADS = 6
GQA_KV_HEADS = 2
GQA_GROUP = GQA_Q_HEADS // GQA_KV_HEADS
Q_BLOCK = 128
ROPE_THETA = 10000.0
POOL_WINDOWS = (2, 4, 8, 16)
POOL_GROUP_DIM = 64
POOL_GROUPS = len(POOL_WINDOWS)
POOL_WIDTH = POOL_GROUPS * POOL_GROUP_DIM

NA_WIDTH = NA_HEADS * HEAD_DIM
GQA_WIDTH = GQA_Q_HEADS * HEAD_DIM
GQA_KV_WIDTH = GQA_KV_HEADS * HEAD_DIM
N_BRANCH = 3
D_FF = -(-8 * D_MODEL // (3 * 256)) * 256
N_MOD = 6
EPS = 1e-6
NEG_INF = -1e30
IN_SIZES = (NA_WIDTH, NA_WIDTH, NA_WIDTH, GQA_WIDTH, GQA_KV_WIDTH, GQA_KV_WIDTH, POOL_WIDTH, N_BRANCH * D_MODEL)
IN_WIDTH = sum(IN_SIZES)

kernel_name = "hybrid_natten_gqa_pool_diffusion_trunk"


def rms_norm(x, g):
    xf = x.astype(jnp.float32)
    y = xf * lax.rsqrt(jnp.mean(xf * xf, axis=-1, keepdims=True) + EPS)
    return (y * g.astype(jnp.float32)).astype(x.dtype)


def modulate(x, shift, scale):
    return x * (1 + scale) + shift


def axial_rope(x, row, col):
    quarter = HEAD_DIM // 4
    half = HEAD_DIM // 2
    freqs = ROPE_THETA ** (-jnp.arange(quarter, dtype=jnp.float32) / quarter)

    def rot(xa, pos):
        ang = pos.astype(jnp.float32)[:, None] * freqs[None, :]
        cos = jnp.cos(ang)[None, :, None, :]
        sin = jnp.sin(ang)[None, :, None, :]
        x1 = xa[..., :quarter].astype(jnp.float32)
        x2 = xa[..., quarter:].astype(jnp.float32)
        return jnp.concatenate([x1 * cos - x2 * sin, x2 * cos + x1 * sin], axis=-1)

    out = jnp.concatenate([rot(x[..., :half], row), rot(x[..., half:], col)], axis=-1)
    return out.astype(x.dtype)


def dense_attention(q, k, v):
    s = jnp.einsum('bqhgd,bkhd->bhgqk', q, k).astype(jnp.float32) * (HEAD_DIM ** -0.5)
    p = jax.nn.softmax(s, axis=-1).astype(v.dtype)
    return jnp.einsum('bhgqk,bkhd->bqhgd', p, v)


def gqa_block_attention(q, k, v):
    B, S = q.shape[0], q.shape[1]
    nb = S // Q_BLOCK
    qb = q.reshape(B, nb, Q_BLOCK, GQA_KV_HEADS, GQA_GROUP, HEAD_DIM).swapaxes(0, 1)
    out = lax.map(lambda qi: dense_attention(qi, k, v), qb)
    return out.swapaxes(0, 1).reshape(B, S, GQA_WIDTH)


def neighborhood_attention(q, k, v, k_ctx, v_ctx, rpb):
    B, S, H, dh = q.shape
    rows = S // GRID_W
    wr = min(NA_WIN_ROWS, rows)
    qg = q.reshape(B, rows, GRID_W, H, dh)
    kg = k.reshape(B, rows, GRID_W, H, dh)
    vg = v.reshape(B, rows, GRID_W, H, dh)
    cq = jnp.arange(GRID_W)
    col_start = jnp.clip(cq - NA_WIN_COLS // 2, 0, GRID_W - NA_WIN_COLS)
    col_valid = (cq[None, :] >= col_start[:, None]) & (cq[None, :] < col_start[:, None] + NA_WIN_COLS)
    dc_idx = jnp.clip(cq[None, :] - cq[:, None] + NA_WIN_COLS - 1, 0, 2 * NA_WIN_COLS - 2)
    scale = dh ** -0.5

    def row_step(r):
        rs = jnp.clip(r - wr // 2, 0, rows - wr)
        k_rows = lax.dynamic_slice_in_dim(kg, rs, wr, axis=1)
        v_rows = lax.dynamic_slice_in_dim(vg, rs, wr, axis=1)
        q_row = lax.dynamic_index_in_dim(qg, r, axis=1, keepdims=False)
        s_loc = jnp.einsum('bqhd,brkhd->bhqrk', q_row, k_rows).astype(jnp.float32) * scale
        dr = rs + jnp.arange(wr) - r + NA_WIN_ROWS - 1
        bias = rpb[:, dr[None, :, None], dc_idx[:, None, :]].astype(jnp.float32)
        s_loc = jnp.where(col_valid[:, None, :], s_loc + bias[None], NEG_INF)
        s_ctx = jnp.einsum('bqhd,blhd->bhql', q_row, k_ctx).astype(jnp.float32) * scale
        s = jnp.concatenate([s_loc.reshape(B, H, GRID_W, wr * GRID_W), s_ctx], axis=-1)
        p = jax.nn.softmax(s, axis=-1).astype(v.dtype)
        p_loc = p[..., :wr * GRID_W].reshape(B, H, GRID_W, wr, GRID_W)
        p_ctx = p[..., wr * GRID_W:]
        return (jnp.einsum('bhqrk,brkhd->bqhd', p_loc, v_rows)
                + jnp.einsum('bhql,blhd->bqhd', p_ctx, v_ctx))

    out = lax.map(row_step, jnp.arange(rows))
    return jnp.moveaxis(out, 0, 1).reshape(B, S, H * dh)


def multiscale_pool(v, w_pool, scale):
    B, N, _ = v.shape
    vf = v.astype(jnp.float32)
    cs = jnp.concatenate([jnp.zeros((B, 1, POOL_WIDTH), jnp.float32), jnp.cumsum(vf, axis=1)], axis=1)
    t = jnp.arange(N)
    means = []
    for g, win in enumerate(POOL_WINDOWS):
        lo = jnp.clip(t - win // 2, 0, N)
        hi = jnp.clip(t + win // 2, 0, N)
        csg = cs[..., g * POOL_GROUP_DIM:(g + 1) * POOL_GROUP_DIM]
        means.append((csg[:, hi] - csg[:, lo]) / (hi - lo).astype(jnp.float32)[None, :, None])
    pooled = (jnp.concatenate(means, axis=-1) - vf).astype(v.dtype)
    pooled = pooled.reshape(B, N, POOL_GROUPS, POOL_GROUP_DIM)
    y = jnp.einsum('bngc,gcd->bngd', pooled, w_pool).reshape(B, N, POOL_WIDTH)
    return y * scale


def project(u, w_in, qn_a, kn_a, qn_b, kn_b):
    B, N, _ = u.shape
    z = u @ w_in
    pieces = []
    off = 0
    for sz in IN_SIZES:
        pieces.append(z[..., off:off + sz])
        off += sz
    qa, ka, va, qb, kb, vb, pc, gates = pieces
    qa = rms_norm(qa.reshape(B, N, NA_HEADS, HEAD_DIM), qn_a)
    ka = rms_norm(ka.reshape(B, N, NA_HEADS, HEAD_DIM), kn_a)
    va = va.reshape(B, N, NA_HEADS, HEAD_DIM)
    qb = rms_norm(qb.reshape(B, N, GQA_Q_HEADS, HEAD_DIM), qn_b)
    kb = rms_norm(kb.reshape(B, N, GQA_KV_HEADS, HEAD_DIM), kn_b)
    vb = vb.reshape(B, N, GQA_KV_HEADS, HEAD_DIM)
    return qa, ka, va, qb, kb, vb, pc, gates


def merge_branches(ya, yb, yc, gates, w_br_a, w_br_b, w_br_c, w_out):
    g = jax.nn.sigmoid(gates)
    ga = g[..., :D_MODEL]
    gb = g[..., D_MODEL:2 * D_MODEL]
    gc = g[..., 2 * D_MODEL:]
    m = ga * (ya @ w_br_a) + gb * (yb @ w_br_b) + gc * (yc @ w_br_c)
    return m @ w_out


def swiglu(u, w1, w3, w2):
    return (jax.nn.silu(u @ w1) * (u @ w3)) @ w2


def setup_inputs(seed: int = 0) -> dict:
    key = jax.random.key(seed)
    ks = jax.random.split(key, 32)
    f32 = jnp.float32

    def nrm(k, shape, scale):
        return jax.random.normal(k, shape, f32) * scale

    L = DEPTH
    D = D_MODEL
    return {
        "x": nrm(ks[0], (BATCH, SEQ, D), 1.0),
        "c": nrm(ks[1], (BATCH, D), 1.0),
        "ctx": nrm(ks[2], (BATCH, CTX_LEN, D), 1.0),
        "c_ctx": nrm(ks[3], (D,), 1.0),
        "w_mod": nrm(ks[4], (L, D, N_MOD * D), 0.5 * D ** -0.5),
        "b_mod": nrm(ks[5], (L, N_MOD * D), 0.01),
        "norm1_g": 1.0 + nrm(ks[6], (L, D), 0.02),
        "norm2_g": 1.0 + nrm(ks[7], (L, D), 0.02),
        "w_in": nrm(ks[8], (L, D, IN_WIDTH), D ** -0.5),
        "q_norm_a": 1.0 + nrm(ks[9], (L, HEAD_DIM), 0.02),
        "k_norm_a": 1.0 + nrm(ks[10], (L, HEAD_DIM), 0.02),
        "q_norm_b": 1.0 + nrm(ks[11], (L, HEAD_DIM), 0.02),
        "k_norm_b": 1.0 + nrm(ks[12], (L, HEAD_DIM), 0.02),
        "rpb_a": nrm(ks[13], (L, NA_HEADS, 2 * NA_WIN_ROWS - 1, 2 * NA_WIN_COLS - 1), 0.2),
        "w_pool": nrm(ks[14], (L, POOL_GROUPS, POOL_GROUP_DIM, POOL_GROUP_DIM), POOL_GROUP_DIM ** -0.5),
        "pool_scale": 1.0 + nrm(ks[15], (L, POOL_WIDTH), 0.1),
        "w_br_a": nrm(ks[16], (L, NA_WIDTH, D), NA_WIDTH ** -0.5),
        "w_br_b": nrm(ks[17], (L, GQA_WIDTH, D), GQA_WIDTH ** -0.5),
        "w_br_c": nrm(ks[18], (L, POOL_WIDTH, D), POOL_WIDTH ** -0.5),
        "w_out": nrm(ks[19], (L, D, D), D ** -0.5),
        "w_ff1": nrm(ks[20], (L, D, D_FF), D ** -0.5),
        "w_ff3": nrm(ks[21], (L, D, D_FF), D ** -0.5),
        "w_ff2": nrm(ks[22], (L, D_FF, D), D_FF ** -0.5),
    }


def reference(x, c, ctx, c_ctx, w_mod, b_mod, norm1_g, norm2_g, w_in, q_norm_a, k_norm_a,
              q_norm_b, k_norm_b, rpb_a, w_pool, pool_scale, w_br_a, w_br_b, w_br_c, w_out,
              w_ff1, w_ff3, w_ff2):
    B, S, _ = x.shape
    Lc = ctx.shape[1]
    t = jnp.arange(S)
    row = t // GRID_W
    col = t % GRID_W
    h = ctx
    sc = jax.nn.silu(c)
    sh = jax.nn.silu(c_ctx)
    for l in range(DEPTH):
        last = l == DEPTH - 1
        mod_x = (sc @ w_mod[l] + b_mod[l])[:, None, :]
        mod_h = sh @ w_mod[l] + b_mod[l]
        sx1, cx1, gx1, sx2, cx2, gx2 = jnp.split(mod_x, N_MOD, axis=-1)
        sh1, ch1, gh1, sh2, ch2, gh2 = jnp.split(mod_h, N_MOD, axis=-1)

        ux = modulate(rms_norm(x, norm1_g[l]), sx1, cx1)
        uh = modulate(rms_norm(h, norm1_g[l]), sh1, ch1)
        qa, ka, va, qb, kb, vb, pc, gates = project(ux, w_in[l], q_norm_a[l], k_norm_a[l], q_norm_b[l], k_norm_b[l])
        qa_h, ka_h, va_h, qb_h, kb_h, vb_h, pc_h, gates_h = project(uh, w_in[l], q_norm_a[l], k_norm_a[l], q_norm_b[l], k_norm_b[l])

        ya = neighborhood_attention(qa, ka, va, ka_h, va_h, rpb_a[l])
        qb_r = axial_rope(qb, row, col).reshape(B, S, GQA_KV_HEADS, GQA_GROUP, HEAD_DIM)
        kb_r = axial_rope(kb, row, col)
        yb = gqa_block_attention(qb_r, jnp.concatenate([kb_h, kb_r], axis=1),
                                 jnp.concatenate([vb_h, vb], axis=1))
        yc = multiscale_pool(pc, w_pool[l], pool_scale[l])
        x = x + gx1 * merge_branches(ya, yb, yc, gates, w_br_a[l], w_br_b[l], w_br_c[l], w_out[l])

        u2 = modulate(rms_norm(x, norm2_g[l]), sx2, cx2)
        x = x + gx2 * swiglu(u2, w_ff1[l], w_ff3[l], w_ff2[l])

        if not last:
            ya_h = dense_attention(qa_h[:, :, :, None, :], ka_h, va_h).reshape(B, Lc, NA_WIDTH)
            yb_h = dense_attention(qb_h.reshape(B, Lc, GQA_KV_HEADS, GQA_GROUP, HEAD_DIM), kb_h, vb_h).reshape(B, Lc, GQA_WIDTH)
            yc_h = multiscale_pool(pc_h, w_pool[l], pool_scale[l])
            h = h + gh1 * merge_branches(ya_h, yb_h, yc_h, gates_h, w_br_a[l], w_br_b[l], w_br_c[l], w_out[l])
            u2h = modulate(rms_norm(h, norm2_g[l]), sh2, ch2)
            h = h + gh2 * swiglu(u2h, w_ff1[l], w_ff3[l], w_ff2[l])
    return x
```

```python
import functools

import numpy as np
import jax
import jax.numpy as jnp
from jax import lax
from jax.experimental import pallas as pl
from jax.experimental.pallas import tpu as pltpu

GRID_W = 64
HEAD_DIM = 64
NA_HEADS = 6
NA_WIN_ROWS = 8
NA_WIN_COLS = 16
GQA_Q_HEADS = 6
GQA_KV_HEADS = 2
GQA_GROUP = GQA_Q_HEADS // GQA_KV_HEADS
ROPE_THETA = 10000.0
POOL_WINDOWS = (2, 4, 8, 16)
POOL_GROUP_DIM = 64
POOL_WIDTH = len(POOL_WINDOWS) * POOL_GROUP_DIM
NA_WIDTH = NA_HEADS * HEAD_DIM
GQA_WIDTH = GQA_Q_HEADS * HEAD_DIM
GQA_KV_WIDTH = GQA_KV_HEADS * HEAD_DIM
N_MOD = 6
EPS = 1e-6
NEG_INF = -1e30
ATTN_SCALE = HEAD_DIM ** -0.5

LANES = 128
SLAB = 2 * HEAD_DIM
MXU_DIM = 256
VMEM_LIMIT = 56 * 1024 * 1024

NA_Q_ROWS = 4
NA_K_ROWS = NA_Q_ROWS + NA_WIN_ROWS - 1
NORM_WIDTH = 2 * NA_WIDTH + GQA_WIDTH + GQA_KV_WIDTH
PROJ_WIDTH = NORM_WIDTH + NA_WIDTH + GQA_KV_WIDTH + POOL_WIDTH
MOD_ROWS = 40

BF16 = jnp.bfloat16
F32 = jnp.float32


def _dot(a, b):
    return jnp.dot(a, b, preferred_element_type=F32)


def _dot_nt(a, b):
    return lax.dot_general(a, b, (((1,), (1,)), ((), ())), preferred_element_type=F32)


def _sigmoid(x):
    return 1.0 / (1.0 + jnp.exp(-x))


def _cparams(n_grid):
    return pltpu.CompilerParams(dimension_semantics=("parallel",) * n_grid, vmem_limit_bytes=VMEM_LIMIT)


def _const_spec(shape):
    nd = len(shape)
    return pl.BlockSpec(shape, lambda *_: (0,) * nd, pipeline_mode=pl.Buffered(1))


def _modulated_norm(xt, gain, shift, scale):
    ms = jnp.mean(xt * xt, axis=-1, keepdims=True)
    y = xt * lax.rsqrt(ms + EPS) * gain
    return y * (1.0 + scale) + shift


def _mod_kernel(c_ref, w_ref, b_ref, o_ref):
    cv = c_ref[...]
    sc = cv * _sigmoid(cv)
    o_ref[0] = _dot(sc.astype(BF16), w_ref[0].astype(BF16)) + b_ref[0]


def _mod_call(c_all, w_mod, b_mod):
    depth, d, nmod = w_mod.shape
    tn = nmod // 4
    return pl.pallas_call(
        _mod_kernel,
        grid=(depth, nmod // tn),
        in_specs=[
            pl.BlockSpec((MOD_ROWS, d), lambda l, j: (0, 0)),
            pl.BlockSpec((1, d, tn), lambda l, j: (l, 0, j)),
            pl.BlockSpec((1, 1, tn), lambda l, j: (l, 0, j)),
        ],
        out_specs=pl.BlockSpec((1, MOD_ROWS, tn), lambda l, j: (l, 0, j)),
        out_shape=jax.ShapeDtypeStruct((depth, MOD_ROWS, nmod), F32),
        compiler_params=_cparams(2),
        name="mod",
    )(c_all, w_mod, b_mod.reshape(depth, 1, nmod))


def _proj_kernel(*refs, rope):
    if rope:
        (x_ref, mod_ref, g1_ref, w_ref, gsum_ref, hg_ref, cos_ref, sina_ref, sinb_ref,
         qa_ref, ka_ref, va_ref, qb_ref, kb_ref, vb_ref, pc_ref) = refs
    else:
        (x_ref, mod_ref, g1_ref, w_ref, gsum_ref, hg_ref,
         qa_ref, ka_ref, va_ref, qb_ref, kb_ref, vb_ref, pc_ref) = refs
    d = x_ref.shape[-1]
    mod = mod_ref[0]
    u = _modulated_norm(x_ref[0], g1_ref[...], mod[:, :d], mod[:, d:2 * d])
    z = _dot(u.astype(BF16), w_ref[...])

    chunks = []
    for c in range(NORM_WIDTH // MXU_DIM):
        zc = z[:, c * MXU_DIM:(c + 1) * MXU_DIM]
        ssq = _dot((zc * zc).astype(BF16), gsum_ref[...])
        r = lax.rsqrt(ssq * (1.0 / HEAD_DIM) + EPS)
        zn = zc * r * hg_ref[:, c * MXU_DIM:(c + 1) * MXU_DIM]
        if rope and c * MXU_DIM >= 2 * NA_WIDTH:
            zn = (zn * cos_ref[...]
                  + pltpu.roll(zn, MXU_DIM - HEAD_DIM // 4, axis=1) * sina_ref[...]
                  + pltpu.roll(zn, HEAD_DIM // 4, axis=1) * sinb_ref[...])
        chunks.append(zn.astype(BF16))

    qa_ref[0, :, 0:256] = chunks[0]
    qa_ref[0, :, 256:384] = chunks[1][:, :LANES]
    ka_ref[0, :, 0:128] = chunks[1][:, LANES:]
    ka_ref[0, :, 128:384] = chunks[2]
    qb_ref[0, :, 0:256] = chunks[3]
    qb_ref[0, :, 256:384] = chunks[4][:, :LANES]
    kb_ref[0] = chunks[4][:, LANES:]
    o = NORM_WIDTH
    va_ref[0] = z[:, o:o + NA_WIDTH].astype(BF16)
    o += NA_WIDTH
    vb_ref[0] = z[:, o:o + GQA_KV_WIDTH].astype(BF16)
    o += GQA_KV_WIDTH
    pc_ref[0] = z[:, o:o + POOL_WIDTH]


def _proj_call(x, mod_l, mod_row_fn, g1, w_proj, gsum, head_gain, rope_tabs, tm):
    b, n, d = x.shape
    rope = rope_tabs is not None
    in_specs = [
        pl.BlockSpec((1, tm, d), lambda bi, i: (bi, i, 0)),
        pl.BlockSpec((1, 1, 2 * d), lambda bi, i: (mod_row_fn(bi), 0, 0)),
        _const_spec((1, d)),
        _const_spec((d, PROJ_WIDTH)),
        _const_spec((MXU_DIM, MXU_DIM)),
        _const_spec((1, NORM_WIDTH)),
    ]
    args = [x, mod_l, g1, w_proj, gsum, head_gain]
    if rope:
        in_specs += [pl.BlockSpec((tm, MXU_DIM), lambda bi, i: (i, 0))] * 3
        args += list(rope_tabs)

    def ospec(width):
        return pl.BlockSpec((1, tm, width), lambda bi, i: (bi, i, 0))

    def oshape(width, dt):
        return jax.ShapeDtypeStruct((b, n, width), dt)

    widths = [(NA_WIDTH, BF16), (NA_WIDTH, BF16), (NA_WIDTH, BF16), (GQA_WIDTH, BF16),
              (GQA_KV_WIDTH, BF16), (GQA_KV_WIDTH, BF16), (POOL_WIDTH, F32)]
    return pl.pallas_call(
        functools.partial(_proj_kernel, rope=rope),
        grid=(b, n // tm),
        in_specs=in_specs,
        out_specs=[ospec(w) for w, _ in widths],
        out_shape=[oshape(w, dt) for w, dt in widths],
        compiler_params=_cparams(2),
        name="proj_rope" if rope else "proj",
    )(*args)


def _half_mask(q, half):
    lane = lax.broadcasted_iota(jnp.int32, (1, SLAB), 1)
    keep = (lane < HEAD_DIM) if half == 0 else (lane >= HEAD_DIM)
    return jnp.where(keep, q, jnp.zeros_like(q))


def _softmax_pv(score_blocks, value_blocks):
    m = score_blocks[0].max(axis=-1, keepdims=True)
    for s in score_blocks[1:]:
        m = jnp.maximum(m, s.max(axis=-1, keepdims=True))
    den = None
    acc = None
    for s, v in zip(score_blocks, value_blocks):
        e = jnp.exp(s - m)
        ssum = e.sum(axis=-1, keepdims=True)
        pv = _dot(e.astype(BF16), v)
        den = ssum if den is None else den + ssum
        acc = pv if acc is None else acc + pv
    return acc / den


def _merge_halves(o0, o1):
    lane = lax.broadcasted_iota(jnp.int32, (1, SLAB), 1)
    return jnp.where(lane < HEAD_DIM, o0, o1)


def _na_kernel(q_ref, k_ref, v_ref, kc_ref, vc_ref, bias_ref, o_ref):
    i = pl.program_id(1)
    n_steps = pl.num_programs(1)
    rows = k_ref.shape[1] // GRID_W
    kstart = jnp.clip(i * NA_Q_ROWS - NA_WIN_ROWS // 2, 0, rows - NA_K_ROWS)
    cfg = jnp.where(i == 0, 0, jnp.where(i == n_steps - 1, 2, 1))
    k0 = pl.multiple_of(kstart * GRID_W, GRID_W)
    nk = NA_K_ROWS * GRID_W
    for p in range(NA_HEADS // 2):
        sl = slice(p * SLAB, (p + 1) * SLAB)
        q = q_ref[0, :, sl]
        kl = k_ref[0, pl.ds(k0, nk), sl]
        vl = v_ref[0, pl.ds(k0, nk), sl]
        kc = kc_ref[0, :, sl]
        vc = vc_ref[0, :, sl]
        outs = []
        for half in range(2):
            qm = _half_mask(q, half)
            s_loc = _dot_nt(qm, kl) + bias_ref[cfg, 2 * p + half]
            s_ctx = _dot_nt(qm, kc)
            outs.append(_softmax_pv([s_loc, s_ctx], [vl, vc]))
        o_ref[0, :, sl] = _merge_halves(outs[0], outs[1]).astype(BF16)


def _na_call(qa, ka, va, ka_h, va_h, bias):
    b, s, w = qa.shape
    lc = ka_h.shape[1]
    tq = NA_Q_ROWS * GRID_W
    return pl.pallas_call(
        _na_kernel,
        grid=(b, s // tq),
        in_specs=[
            pl.BlockSpec((1, tq, w), lambda bi, i: (bi, i, 0)),
            pl.BlockSpec((1, s, w), lambda bi, i: (bi, 0, 0)),
            pl.BlockSpec((1, s, w), lambda bi, i: (bi, 0, 0)),
            pl.BlockSpec((1, lc, w), lambda bi, i: (bi, 0, 0)),
            pl.BlockSpec((1, lc, w), lambda bi, i: (bi, 0, 0)),
            _const_spec(bias.shape),
        ],
        out_specs=pl.BlockSpec((1, tq, w), lambda bi, i: (bi, i, 0)),
        out_shape=jax.ShapeDtypeStruct((b, s, w), BF16),
        compiler_params=_cparams(2),
        name="na",
    )(qa, ka, va, ka_h, va_h, bias)


def _na_bias_table(rpb):
    rows = 2048 // GRID_W
    last_r0 = rows - NA_Q_ROWS
    geoms = [(0, 0), (2 * NA_Q_ROWS, 2 * NA_Q_ROWS - NA_WIN_ROWS // 2), (last_r0, rows - NA_K_ROWS)]
    i = np.arange(NA_Q_ROWS)[:, None, None, None]
    qc = np.arange(GRID_W)[None, :, None, None]
    j = np.arange(NA_K_ROWS)[None, None, :, None]
    kc = np.arange(GRID_W)[None, None, None, :]
    cs = np.clip(qc - NA_WIN_COLS // 2, 0, GRID_W - NA_WIN_COLS)
    col_valid = (kc >= cs) & (kc < cs + NA_WIN_COLS)
    dc = np.clip(kc - qc + NA_WIN_COLS - 1, 0, 2 * NA_WIN_COLS - 2)
    tabs = []
    for r0, ks in geoms:
        r = r0 + i
        kr = ks + j
        rs = np.clip(r - NA_WIN_ROWS // 2, 0, rows - NA_WIN_ROWS)
        row_valid = (kr >= rs) & (kr < rs + NA_WIN_ROWS)
        dr = np.clip(kr - r + NA_WIN_ROWS - 1, 0, 2 * NA_WIN_ROWS - 2)
        full = (NA_Q_ROWS, GRID_W, NA_K_ROWS, GRID_W)
        valid = np.broadcast_to(row_valid & col_valid, full)
        vals = rpb[:, np.broadcast_to(dr, full), np.broadcast_to(dc, full)]
        t = jnp.where(valid[None], vals.astype(F32), NEG_INF)
        tabs.append(t.reshape(rpb.shape[0], NA_Q_ROWS * GRID_W, NA_K_ROWS * GRID_W))
    return jnp.stack(tabs)


def _gqa_kernel(q_ref, k_ref, v_ref, kc_ref, vc_ref, o_ref):
    kx = k_ref[0]
    vx = v_ref[0]
    kc = kc_ref[0]
    vc = vc_ref[0]
    for g in range(GQA_GROUP):
        sl = slice(g * SLAB, (g + 1) * SLAB)
        q = q_ref[0, :, sl]
        outs = []
        for half in range(2):
            qm = _half_mask(q, half)
            outs.append(_softmax_pv([_dot_nt(qm, kc), _dot_nt(qm, kx)], [vc, vx]))
        o_ref[0, :, sl] = _merge_halves(outs[0], outs[1]).astype(BF16)


def _gqa_call(qb, kb, vb, kb_h, vb_h, tq):
    b, s, w = qb.shape
    lc = kb_h.shape[1]
    kw = kb.shape[-1]
    return pl.pallas_call(
        _gqa_kernel,
        grid=(b, s // tq),
        in_specs=[
            pl.BlockSpec((1, tq, w), lambda bi, i: (bi, i, 0)),
            pl.BlockSpec((1, s, kw), lambda bi, i: (bi, 0, 0)),
            pl.BlockSpec((1, s, kw), lambda bi, i: (bi, 0, 0)),
            pl.BlockSpec((1, lc, kw), lambda bi, i: (bi, 0, 0)),
            pl.BlockSpec((1, lc, kw), lambda bi, i: (bi, 0, 0)),
        ],
        out_specs=pl.BlockSpec((1, tq, w), lambda bi, i: (bi, i, 0)),
        out_shape=jax.ShapeDtypeStruct((b, s, w), BF16),
        compiler_params=_cparams(2),
        name="gqa",
    )(qb, kb, vb, kb_h, vb_h)


def _cattn_kernel(qa_ref, ka_ref, va_ref, qb_ref, kb_ref, vb_ref, ya_ref, yb_ref):
    for p in range(NA_HEADS // 2):
        sl = slice(p * SLAB, (p + 1) * SLAB)
        q = qa_ref[0, :, sl]
        k = ka_ref[0, :, sl]
        v = va_ref[0, :, sl]
        outs = [_softmax_pv([_dot_nt(_half_mask(q, h), k)], [v]) for h in range(2)]
        ya_ref[0, :, sl] = _merge_halves(outs[0], outs[1]).astype(BF16)
    k = kb_ref[0]
    v = vb_ref[0]
    for g in range(GQA_GROUP):
        sl = slice(g * SLAB, (g + 1) * SLAB)
        q = qb_ref[0, :, sl]
        outs = [_softmax_pv([_dot_nt(_half_mask(q, h), k)], [v]) for h in range(2)]
        yb_ref[0, :, sl] = _merge_halves(outs[0], outs[1]).astype(BF16)


def _cattn_call(qa, ka, va, qb, kb, vb):
    b, lc, w = qa.shape
    kw = kb.shape[-1]

    def spec(width):
        return pl.BlockSpec((1, lc, width), lambda bi: (bi, 0, 0))

    return pl.pallas_call(
        _cattn_kernel,
        grid=(b,),
        in_specs=[spec(w), spec(w), spec(w), spec(w), spec(kw), spec(kw)],
        out_specs=[spec(w), spec(w)],
        out_shape=[jax.ShapeDtypeStruct((b, lc, w), BF16)] * 2,
        compiler_params=_cparams(1),
        name="cattn",
    )(qa, ka, va, qb, kb, vb)


POOL_PAD = 8


def _pool_kernel(pc_ref, w_ref, scale_ref, o_ref, pad_ref):
    n = pc_ref.shape[1]
    zeros = jnp.zeros((POOL_PAD, POOL_WIDTH), F32)
    pad_ref[0:POOL_PAD, :] = zeros
    pad_ref[POOL_PAD + n:2 * POOL_PAD + n, :] = zeros
    pad_ref[POOL_PAD:POOL_PAD + n, :] = pc_ref[0]

    t = lax.broadcasted_iota(jnp.int32, (n, 1), 0)
    lane = lax.broadcasted_iota(jnp.int32, (1, LANES), 1)
    first = lane < POOL_GROUP_DIM

    def count(win):
        lo = jnp.maximum(t - win // 2, 0)
        hi = jnp.minimum(t + win // 2, n)
        return (hi - lo).astype(F32)

    cols = []
    for col in range(POOL_WIDTH // LANES):
        w_small, w_big = POOL_WINDOWS[2 * col], POOL_WINDOWS[2 * col + 1]
        csl = slice(col * LANES, (col + 1) * LANES)

        def shifted(jj):
            return pad_ref[pl.ds(POOL_PAD + jj, n), csl]

        small = shifted(-w_small // 2)
        for jj in range(-w_small // 2 + 1, w_small // 2):
            small = small + shifted(jj)
        big = small
        for jj in list(range(-w_big // 2, -w_small // 2)) + list(range(w_small // 2, w_big // 2)):
            big = big + shifted(jj)
        total = jnp.where(first, small, big)
        cnt = jnp.where(first, count(w_small), count(w_big))
        cols.append(total / cnt - shifted(0))
    pooled = jnp.concatenate(cols, axis=-1).astype(BF16)
    o_ref[0] = (_dot(pooled, w_ref[...]) * scale_ref[...]).astype(BF16)


def _pool_call(pc, w_blockdiag, scale):
    b, n, w = pc.shape
    return pl.pallas_call(
        _pool_kernel,
        grid=(b,),
        in_specs=[
            pl.BlockSpec((1, n, w), lambda bi: (bi, 0, 0)),
            _const_spec((w, w)),
            _const_spec((1, w)),
        ],
        out_specs=pl.BlockSpec((1, n, w), lambda bi: (bi, 0, 0)),
        out_shape=jax.ShapeDtypeStruct((b, n, w), BF16),
        scratch_shapes=[pltpu.VMEM((n + 2 * POOL_PAD, w), F32)],
        compiler_params=_cparams(1),
        name="pool",
    )(pc, w_blockdiag, scale)


def _merge_kernel(x_ref, mod_ref, g1_ref, ya_ref, yb_ref, yc_ref,
                  wg_ref, wa_ref, wb_ref, wc_ref, wo_ref, o_ref):
    d = x_ref.shape[-1]
    xt = x_ref[0]
    mod = mod_ref[0]
    u = _modulated_norm(xt, g1_ref[...], mod[:, :d], mod[:, d:2 * d]).astype(BF16)
    m = None
    for br, (y_ref, w_ref) in enumerate(((ya_ref, wa_ref), (yb_ref, wb_ref), (yc_ref, wc_ref))):
        gate = _sigmoid(_dot(u, wg_ref[:, br * d:(br + 1) * d]))
        term = gate * _dot(y_ref[0], w_ref[...])
        m = term if m is None else m + term
    o_ref[0] = xt + mod[:, 2 * d:3 * d] * _dot(m.astype(BF16), wo_ref[...])


def _merge_call(x, mod_l, mod_row_fn, g1, ya, yb, yc, wg, wa, wb, wc, wo, tm):
    b, n, d = x.shape

    def tok(width):
        return pl.BlockSpec((1, tm, width), lambda bi, i: (bi, i, 0))

    return pl.pallas_call(
        _merge_kernel,
        grid=(b, n // tm),
        in_specs=[
            tok(d),
            pl.BlockSpec((1, 1, 3 * d), lambda bi, i: (mod_row_fn(bi), 0, 0)),
            _const_spec((1, d)),
            tok(ya.shape[-1]), tok(yb.shape[-1]), tok(yc.shape[-1]),
            _const_spec(wg.shape), _const_spec(wa.shape), _const_spec(wb.shape),
            _const_spec(wc.shape), _const_spec(wo.shape),
        ],
        out_specs=tok(d),
        out_shape=jax.ShapeDtypeStruct((b, n, d), F32),
        compiler_params=_cparams(2),
        name="merge",
    )(x, mod_l, g1, ya, yb, yc, wg, wa, wb, wc, wo)


def _ffn_kernel(x_ref, mod_ref, g2_ref, w1_ref, w3_ref, w2_ref, o_ref):
    d = x_ref.shape[-1]
    xt = x_ref[0]
    mod = mod_ref[0]
    u = _modulated_norm(xt, g2_ref[...], mod[:, :d], mod[:, d:2 * d]).astype(BF16)
    a = _dot(u, w1_ref[...])
    hid = (a * _sigmoid(a)) * _dot(u, w3_ref[...])
    o_ref[0] = xt + mod[:, 2 * d:3 * d] * _dot(hid.astype(BF16), w2_ref[...])


def _ffn_call(x, mod_l, mod_row_fn, g2, w1, w3, w2, tm):
    b, n, d = x.shape
    tok = pl.BlockSpec((1, tm, d), lambda bi, i: (bi, i, 0))
    return pl.pallas_call(
        _ffn_kernel,
        grid=(b, n // tm),
        in_specs=[
            tok,
            pl.BlockSpec((1, 1, 3 * d), lambda bi, i: (mod_row_fn(bi), 0, 1)),
            _const_spec((1, d)),
            _const_spec(w1.shape), _const_spec(w3.shape), _const_spec(w2.shape),
        ],
        out_specs=tok,
        out_shape=jax.ShapeDtypeStruct((b, n, d), F32),
        compiler_params=_cparams(2),
        name="ffn",
    )(x, mod_l, g2, w1, w3, w2)


def _rope_tables(s):
    quarter = HEAD_DIM // 4
    t = jnp.arange(s)
    row = (t // GRID_W).astype(F32)
    col = (t % GRID_W).astype(F32)
    freqs = ROPE_THETA ** (-jnp.arange(quarter, dtype=F32) / quarter)
    ang_r = row[:, None] * freqs[None, :]
    ang_c = col[:, None] * freqs[None, :]
    zero = jnp.zeros_like(ang_r)
    cos = jnp.concatenate([jnp.cos(ang_r)] * 2 + [jnp.cos(ang_c)] * 2, axis=-1)
    sin_a = jnp.concatenate([-jnp.sin(ang_r), zero, -jnp.sin(ang_c), zero], axis=-1)
    sin_b = jnp.concatenate([zero, jnp.sin(ang_r), zero, jnp.sin(ang_c)], axis=-1)
    reps = MXU_DIM // HEAD_DIM
    return tuple(jnp.tile(a, (1, reps)) for a in (cos, sin_a, sin_b))


def _gqa_head_order():
    return [h for g in range(GQA_GROUP) for h in (g, GQA_GROUP + g)]


def _head_cols(order):
    return np.concatenate([np.arange(h * HEAD_DIM, (h + 1) * HEAD_DIM) for h in order])


def kernel(x, c, ctx, c_ctx, w_mod, b_mod, norm1_g, norm2_g, w_in, q_norm_a, k_norm_a, q_norm_b, k_norm_b,
           rpb_a, w_pool, pool_scale, w_br_a, w_br_b, w_br_c, w_out, w_ff1, w_ff3, w_ff2):
    b, s, d = x.shape
    lc = ctx.shape[1]
    depth = w_mod.shape[0]
    assert b + 1 <= MOD_ROWS and s == 2048 and s % (NA_Q_ROWS * GRID_W) == 0

    c_all = jnp.concatenate([c, c_ctx[None, :], jnp.zeros((MOD_ROWS - b - 1, d), F32)], axis=0)
    mod = _mod_call(c_all, w_mod, b_mod)

    rope_tabs = _rope_tables(s)
    gsum = jnp.kron(jnp.eye(MXU_DIM // HEAD_DIM, dtype=F32), jnp.ones((HEAD_DIM, HEAD_DIM), F32)).astype(BF16)
    perm_b = _head_cols(_gqa_head_order())

    o_qa, o_ka, o_va = 0, NA_WIDTH, 2 * NA_WIDTH
    o_qb = 3 * NA_WIDTH
    o_kb = o_qb + GQA_WIDTH
    o_vb = o_kb + GQA_KV_WIDTH
    o_pc = o_vb + GQA_KV_WIDTH
    o_g = o_pc + POOL_WIDTH
    proj_cols = np.concatenate([
        np.arange(o_qa, o_qa + NA_WIDTH), np.arange(o_ka, o_ka + NA_WIDTH), o_qb + perm_b,
        np.arange(o_kb, o_kb + GQA_KV_WIDTH), np.arange(o_va, o_va + NA_WIDTH),
        np.arange(o_vb, o_vb + GQA_KV_WIDTH), np.arange(o_pc, o_pc + POOL_WIDTH)])

    x_row = lambda bi: bi
    h_row = lambda bi: b
    tm_x, tm_h = 512, lc
    h = ctx
    for l in range(depth):
        last = l == depth - 1
        mod_l = mod[l].reshape(MOD_ROWS, 1, N_MOD * d)
        w_proj = w_in[l][:, proj_cols].astype(BF16)
        w_gate = w_in[l][:, o_g:].astype(BF16)
        head_gain = jnp.concatenate([
            jnp.tile(q_norm_a[l], NA_HEADS) * ATTN_SCALE, jnp.tile(k_norm_a[l], NA_HEADS),
            jnp.tile(q_norm_b[l], GQA_Q_HEADS) * ATTN_SCALE, jnp.tile(k_norm_b[l], GQA_KV_HEADS)])[None, :]
        g1 = norm1_g[l][None, :]
        g2 = norm2_g[l][None, :]
        wa = w_br_a[l].astype(BF16)
        wb = w_br_b[l][perm_b, :].astype(BF16)
        wc = w_br_c[l].astype(BF16)
        wo = w_out[l].astype(BF16)
        w1 = w_ff1[l].astype(BF16)
        w3 = w_ff3[l].astype(BF16)
        w2 = w_ff2[l].astype(BF16)
        wp = jax.scipy.linalg.block_diag(*[w_pool[l, g] for g in range(len(POOL_WINDOWS))]).astype(BF16)
        pscale = pool_scale[l][None, :]
        bias = _na_bias_table(rpb_a[l])

        qa, ka, va, qb, kb, vb, pc = _proj_call(x, mod_l, x_row, g1, w_proj, gsum, head_gain, rope_tabs, tm_x)
        qa_h, ka_h, va_h, qb_h, kb_h, vb_h, pc_h = _proj_call(h, mod_l, h_row, g1, w_proj, gsum, head_gain,
                                                              None, tm_h)
        ya = _na_call(qa, ka, va, ka_h, va_h, bias)
        yb = _gqa_call(qb, kb, vb, kb_h, vb_h, 256)
        yc = _pool_call(pc, wp, pscale)
        x = _merge_call(x, mod_l, x_row, g1, ya, yb, yc, w_gate, wa, wb, wc, wo, tm_x)
        x = _ffn_call(x, mod_l, x_row, g2, w1, w3, w2, tm_x)
        if not last:
            ya_h, yb_h = _cattn_call(qa_h, ka_h, va_h, qb_h, kb_h, vb_h)
            yc_h = _pool_call(pc_h, wp, pscale)
            h = _merge_call(h, mod_l, h_row, g1, ya_h, yb_h, yc_h, w_gate, wa, wb, wc, wo, tm_h)
            h = _ffn_call(h, mod_l, h_row, g2, w1, w3, w2, tm_h)
    return x
```

```python
import functools

import numpy as np
import jax
import jax.numpy as jnp
from jax import lax
from jax.experimental import pallas as pl
from jax.experimental.pallas import tpu as pltpu

GRID_W = 64
HEAD_DIM = 64
NA_HEADS = 6
NA_WIN_ROWS = 8
NA_WIN_COLS = 16
GQA_Q_HEADS = 6
GQA_KV_HEADS = 2
GQA_GROUP = GQA_Q_HEADS // GQA_KV_HEADS
ROPE_THETA = 10000.0
POOL_WINDOWS = (2, 4, 8, 16)
POOL_GROUP_DIM = 64
POOL_WIDTH = len(POOL_WINDOWS) * POOL_GROUP_DIM
NA_WIDTH = NA_HEADS * HEAD_DIM
GQA_WIDTH = GQA_Q_HEADS * HEAD_DIM
GQA_KV_WIDTH = GQA_KV_HEADS * HEAD_DIM
N_MOD = 6
EPS = 1e-6
NEG_INF = -1e30
ATTN_SCALE = HEAD_DIM ** -0.5

LANES = 128
SLAB = 2 * HEAD_DIM
MXU_DIM = 256
VMEM_LIMIT = 56 * 1024 * 1024

NA_Q_ROWS = 4
NA_K_ROWS = NA_Q_ROWS + NA_WIN_ROWS - 1
NORM_WIDTH = 2 * NA_WIDTH + GQA_WIDTH + GQA_KV_WIDTH
PROJ_WIDTH = NORM_WIDTH + NA_WIDTH + GQA_KV_WIDTH + POOL_WIDTH
MOD_ROWS = 40

BF16 = jnp.bfloat16
F32 = jnp.float32


def _dot(a, b):
    return jnp.dot(a, b, preferred_element_type=F32)


def _dot_nt(a, b):
    return lax.dot_general(a, b, (((1,), (1,)), ((), ())), preferred_element_type=F32)


def _sigmoid(x):
    return 1.0 / (1.0 + jnp.exp(-x))


def _cparams(n_grid):
    return pltpu.CompilerParams(dimension_semantics=("parallel",) * n_grid, vmem_limit_bytes=VMEM_LIMIT)


def _const_spec(shape):
    nd = len(shape)
    return pl.BlockSpec(shape, lambda *_: (0,) * nd, pipeline_mode=pl.Buffered(1))


def _modulated_norm(xt, gain, shift, scale):
    ms = jnp.mean(xt * xt, axis=-1, keepdims=True)
    y = xt * lax.rsqrt(ms + EPS) * gain
    return y * (1.0 + scale) + shift


def _mod_kernel(c_ref, w_ref, b_ref, o_ref):
    cv = c_ref[...]
    sc = cv * _sigmoid(cv)
    o_ref[0] = _dot(sc.astype(BF16), w_ref[0].astype(BF16)) + b_ref[0]


def _mod_call(c_all, w_mod, b_mod):
    depth, d, nmod = w_mod.shape
    tn = nmod // 4
    return pl.pallas_call(
        _mod_kernel,
        grid=(depth, nmod // tn),
        in_specs=[
            pl.BlockSpec((MOD_ROWS, d), lambda l, j: (0, 0)),
            pl.BlockSpec((1, d, tn), lambda l, j: (l, 0, j)),
            pl.BlockSpec((1, 1, tn), lambda l, j: (l, 0, j)),
        ],
        out_specs=pl.BlockSpec((1, MOD_ROWS, tn), lambda l, j: (l, 0, j)),
        out_shape=jax.ShapeDtypeStruct((depth, MOD_ROWS, nmod), F32),
        compiler_params=_cparams(2),
        name="mod",
    )(c_all, w_mod, b_mod.reshape(depth, 1, nmod))


def _proj_kernel(*refs, rope):
    if rope:
        (x_ref, mod_ref, g1_ref, w_ref, gsum_ref, hg_ref, cos_ref, sina_ref, sinb_ref,
         qa_ref, ka_ref, va_ref, qb_ref, kb_ref, vb_ref, pc_ref) = refs
    else:
        (x_ref, mod_ref, g1_ref, w_ref, gsum_ref, hg_ref,
         qa_ref, ka_ref, va_ref, qb_ref, kb_ref, vb_ref, pc_ref) = refs
    d = x_ref.shape[-1]
    mod = mod_ref[0]
    u = _modulated_norm(x_ref[0], g1_ref[...], mod[:, :d], mod[:, d:2 * d])
    z = _dot(u.astype(BF16), w_ref[...])

    chunks = []
    for c in range(NORM_WIDTH // MXU_DIM):
        zc = z[:, c * MXU_DIM:(c + 1) * MXU_DIM]
        ssq = _dot((zc * zc).astype(BF16), gsum_ref[...])
        r = lax.rsqrt(ssq * (1.0 / HEAD_DIM) + EPS)
        zn = zc * r * hg_ref[:, c * MXU_DIM:(c + 1) * MXU_DIM]
        if rope and c * MXU_DIM >= 2 * NA_WIDTH:
            zn = (zn * cos_ref[...]
                  + pltpu.roll(zn, MXU_DIM - HEAD_DIM // 4, axis=1) * sina_ref[...]
                  + pltpu.roll(zn, HEAD_DIM // 4, axis=1) * sinb_ref[...])
        chunks.append(zn.astype(BF16))

    qa_ref[0, :, 0:256] = chunks[0]
    qa_ref[0, :, 256:384] = chunks[1][:, :LANES]
    ka_ref[0, :, 0:128] = chunks[1][:, LANES:]
    ka_ref[0, :, 128:384] = chunks[2]
    qb_ref[0, :, 0:256] = chunks[3]
    qb_ref[0, :, 256:384] = chunks[4][:, :LANES]
    kb_ref[0] = chunks[4][:, LANES:]
    o = NORM_WIDTH
    va_ref[0] = z[:, o:o + NA_WIDTH].astype(BF16)
    o += NA_WIDTH
    vb_ref[0] = z[:, o:o + GQA_KV_WIDTH].astype(BF16)
    o += GQA_KV_WIDTH
    pc_ref[0] = z[:, o:o + POOL_WIDTH]


def _proj_call(x, mod_l, mod_row_fn, g1, w_proj, gsum, head_gain, rope_tabs, tm):
    b, n, d = x.shape
    rope = rope_tabs is not None
    in_specs = [
        pl.BlockSpec((1, tm, d), lambda bi, i: (bi, i, 0)),
        pl.BlockSpec((1, 1, 2 * d), lambda bi, i: (mod_row_fn(bi), 0, 0)),
        _const_spec((1, d)),
        _const_spec((d, PROJ_WIDTH)),
        _const_spec((MXU_DIM, MXU_DIM)),
        _const_spec((1, NORM_WIDTH)),
    ]
    args = [x, mod_l, g1, w_proj, gsum, head_gain]
    if rope:
        in_specs += [pl.BlockSpec((tm, MXU_DIM), lambda bi, i: (i, 0))] * 3
        args += list(rope_tabs)

    def ospec(width):
        return pl.BlockSpec((1, tm, width), lambda bi, i: (bi, i, 0))

    def oshape(width, dt):
        return jax.ShapeDtypeStruct((b, n, width), dt)

    widths = [(NA_WIDTH, BF16), (NA_WIDTH, BF16), (NA_WIDTH, BF16), (GQA_WIDTH, BF16),
              (GQA_KV_WIDTH, BF16), (GQA_KV_WIDTH, BF16), (POOL_WIDTH, F32)]
    return pl.pallas_call(
        functools.partial(_proj_kernel, rope=rope),
        grid=(b, n // tm),
        in_specs=in_specs,
        out_specs=[ospec(w) for w, _ in widths],
        out_shape=[oshape(w, dt) for w, dt in widths],
        compiler_params=_cparams(2),
        name="proj_rope" if rope else "proj",
    )(*args)


def _half_mask(q, half):
    lane = lax.broadcasted_iota(jnp.int32, (1, SLAB), 1)
    keep = (lane < HEAD_DIM) if half == 0 else (lane >= HEAD_DIM)
    return jnp.where(keep, q, jnp.zeros_like(q))


def _softmax_pv(score_blocks, value_blocks):
    m = score_blocks[0].max(axis=-1, keepdims=True)
    for s in score_blocks[1:]:
        m = jnp.maximum(m, s.max(axis=-1, keepdims=True))
    den = None
    acc = None
    for s, v in zip(score_blocks, value_blocks):
        e = jnp.exp(s - m)
        ssum = e.sum(axis=-1, keepdims=True)
        pv = _dot(e.astype(BF16), v)
        den = ssum if den is None else den + ssum
        acc = pv if acc is None else acc + pv
    return acc / den


def _merge_halves(o0, o1):
    lane = lax.broadcasted_iota(jnp.int32, (1, SLAB), 1)
    return jnp.where(lane < HEAD_DIM, o0, o1)


def _na_kernel(q_ref, k_ref, v_ref, kc_ref, vc_ref, bias_ref, o_ref):
    i = pl.program_id(1)
    n_steps = pl.num_programs(1)
    rows = k_ref.shape[1] // GRID_W
    kstart = jnp.clip(i * NA_Q_ROWS - NA_WIN_ROWS // 2, 0, rows - NA_K_ROWS)
    cfg = jnp.where(i == 0, 0, jnp.where(i == n_steps - 1, 2, 1))
    k0 = pl.multiple_of(kstart * GRID_W, GRID_W)
    nk = NA_K_ROWS * GRID_W
    for p in range(NA_HEADS // 2):
        sl = slice(p * SLAB, (p + 1) * SLAB)
        q = q_ref[0, :, sl]
        kl = k_ref[0, pl.ds(k0, nk), sl]
        vl = v_ref[0, pl.ds(k0, nk), sl]
        kc = kc_ref[0, :, sl]
        vc = vc_ref[0, :, sl]
        outs = []
        for half in range(2):
            qm = _half_mask(q, half)
            s_loc = _dot_nt(qm, kl) + bias_ref[cfg, 2 * p + half]
            s_ctx = _dot_nt(qm, kc)
            outs.append(_softmax_pv([s_loc, s_ctx], [vl, vc]))
        o_ref[0, :, sl] = _merge_halves(outs[0], outs[1]).astype(BF16)


def _na_call(qa, ka, va, ka_h, va_h, bias):
    b, s, w = qa.shape
    lc = ka_h.shape[1]
    tq = NA_Q_ROWS * GRID_W
    return pl.pallas_call(
        _na_kernel,
        grid=(b, s // tq),
        in_specs=[
            pl.BlockSpec((1, tq, w), lambda bi, i: (bi, i, 0)),
            pl.BlockSpec((1, s, w), lambda bi, i: (bi, 0, 0)),
            pl.BlockSpec((1, s, w), lambda bi, i: (bi, 0, 0)),
            pl.BlockSpec((1, lc, w), lambda bi, i: (bi, 0, 0)),
            pl.BlockSpec((1, lc, w), lambda bi, i: (bi, 0, 0)),
            _const_spec(bias.shape),
        ],
        out_specs=pl.BlockSpec((1, tq, w), lambda bi, i: (bi, i, 0)),
        out_shape=jax.ShapeDtypeStruct((b, s, w), BF16),
        compiler_params=_cparams(2),
        name="na",
    )(qa, ka, va, ka_h, va_h, bias)


def _na_bias_table(rpb):
    rows = 2048 // GRID_W
    heads = rpb.shape[0]
    last_r0 = rows - NA_Q_ROWS
    geoms = [(0, 0), (2 * NA_Q_ROWS, 2 * NA_Q_ROWS - NA_WIN_ROWS // 2), (last_r0, rows - NA_K_ROWS)]
    qc = np.arange(GRID_W)[:, None]
    kc = np.arange(GRID_W)[None, :]
    cs = np.clip(qc - NA_WIN_COLS // 2, 0, GRID_W - NA_WIN_COLS)
    col_valid = (kc >= cs) & (kc < cs + NA_WIN_COLS)
    lpad = GRID_W - NA_WIN_COLS
    rp = jnp.pad(rpb.astype(F32), ((0, 0), (0, 0), (lpad, lpad)))
    toe = jnp.stack([rp[:, :, GRID_W - 1 - q:2 * GRID_W - 1 - q] for q in range(GRID_W)], axis=2)
    toe = jnp.where(col_valid[None, None], toe, NEG_INF)
    dpad = NA_K_ROWS + NA_Q_ROWS
    toe = jnp.pad(toe, ((0, 0), (dpad, dpad), (0, 0), (0, 0)), constant_values=NEG_INF)
    tabs = []
    for r0, ks in geoms:
        blocks = []
        for i in range(NA_Q_ROWS):
            r = r0 + i
            kr = ks + np.arange(NA_K_ROWS)
            rs = int(np.clip(r - NA_WIN_ROWS // 2, 0, rows - NA_WIN_ROWS))
            row_valid = (kr >= rs) & (kr < rs + NA_WIN_ROWS)
            dr0 = ks - r + NA_WIN_ROWS - 1
            blk = toe[:, dpad + dr0:dpad + dr0 + NA_K_ROWS]
            blk = jnp.where(row_valid[None, :, None, None], blk, NEG_INF)
            blocks.append(blk.transpose(0, 2, 1, 3).reshape(heads, GRID_W, NA_K_ROWS * GRID_W))
        tabs.append(jnp.concatenate(blocks, axis=1))
    return jnp.stack(tabs)


def _gqa_kernel(q_ref, k_ref, v_ref, kc_ref, vc_ref, o_ref):
    kx = k_ref[0]
    vx = v_ref[0]
    kc = kc_ref[0]
    vc = vc_ref[0]
    for g in range(GQA_GROUP):
        sl = slice(g * SLAB, (g + 1) * SLAB)
        q = q_ref[0, :, sl]
        outs = []
        for half in range(2):
            qm = _half_mask(q, half)
            outs.append(_softmax_pv([_dot_nt(qm, kc), _dot_nt(qm, kx)], [vc, vx]))
        o_ref[0, :, sl] = _merge_halves(outs[0], outs[1]).astype(BF16)


def _gqa_call(qb, kb, vb, kb_h, vb_h, tq):
    b, s, w = qb.shape
    lc = kb_h.shape[1]
    kw = kb.shape[-1]
    return pl.pallas_call(
        _gqa_kernel,
        grid=(b, s // tq),
        in_specs=[
            pl.BlockSpec((1, tq, w), lambda bi, i: (bi, i, 0)),
            pl.BlockSpec((1, s, kw), lambda bi, i: (bi, 0, 0)),
            pl.BlockSpec((1, s, kw), lambda bi, i: (bi, 0, 0)),
            pl.BlockSpec((1, lc, kw), lambda bi, i: (bi, 0, 0)),
            pl.BlockSpec((1, lc, kw), lambda bi, i: (bi, 0, 0)),
        ],
        out_specs=pl.BlockSpec((1, tq, w), lambda bi, i: (bi, i, 0)),
        out_shape=jax.ShapeDtypeStruct((b, s, w), BF16),
        compiler_params=_cparams(2),
        name="gqa",
    )(qb, kb, vb, kb_h, vb_h)


def _cattn_kernel(qa_ref, ka_ref, va_ref, qb_ref, kb_ref, vb_ref, ya_ref, yb_ref):
    for p in range(NA_HEADS // 2):
        sl = slice(p * SLAB, (p + 1) * SLAB)
        q = qa_ref[0, :, sl]
        k = ka_ref[0, :, sl]
        v = va_ref[0, :, sl]
        outs = [_softmax_pv([_dot_nt(_half_mask(q, h), k)], [v]) for h in range(2)]
        ya_ref[0, :, sl] = _merge_halves(outs[0], outs[1]).astype(BF16)
    k = kb_ref[0]
    v = vb_ref[0]
    for g in range(GQA_GROUP):
        sl = slice(g * SLAB, (g + 1) * SLAB)
        q = qb_ref[0, :, sl]
        outs = [_softmax_pv([_dot_nt(_half_mask(q, h), k)], [v]) for h in range(2)]
        yb_ref[0, :, sl] = _merge_halves(outs[0], outs[1]).astype(BF16)


def _cattn_call(qa, ka, va, qb, kb, vb):
    b, lc, w = qa.shape
    kw = kb.shape[-1]

    def spec(width):
        return pl.BlockSpec((1, lc, width), lambda bi: (bi, 0, 0))

    return pl.pallas_call(
        _cattn_kernel,
        grid=(b,),
        in_specs=[spec(w), spec(w), spec(w), spec(w), spec(kw), spec(kw)],
        out_specs=[spec(w), spec(w)],
        out_shape=[jax.ShapeDtypeStruct((b, lc, w), BF16)] * 2,
        compiler_params=_cparams(1),
        name="cattn",
    )(qa, ka, va, qb, kb, vb)


POOL_PAD = 8


def _pool_kernel(pc_ref, w_ref, scale_ref, o_ref, pad_ref):
    n = pc_ref.shape[1]
    zeros = jnp.zeros((POOL_PAD, POOL_WIDTH), F32)
    pad_ref[0:POOL_PAD, :] = zeros
    pad_ref[POOL_PAD + n:2 * POOL_PAD + n, :] = zeros
    pad_ref[POOL_PAD:POOL_PAD + n, :] = pc_ref[0]

    t = lax.broadcasted_iota(jnp.int32, (n, 1), 0)
    lane = lax.broadcasted_iota(jnp.int32, (1, LANES), 1)
    first = lane < POOL_GROUP_DIM

    def count(win):
        lo = jnp.maximum(t - win // 2, 0)
        hi = jnp.minimum(t + win // 2, n)
        return (hi - lo).astype(F32)

    cols = []
    for col in range(POOL_WIDTH // LANES):
        w_small, w_big = POOL_WINDOWS[2 * col], POOL_WINDOWS[2 * col + 1]
        csl = slice(col * LANES, (col + 1) * LANES)

        def shifted(jj):
            return pad_ref[pl.ds(POOL_PAD + jj, n), csl]

        small = shifted(-w_small // 2)
        for jj in range(-w_small // 2 + 1, w_small // 2):
            small = small + shifted(jj)
        big = small
        for jj in list(range(-w_big // 2, -w_small // 2)) + list(range(w_small // 2, w_big // 2)):
            big = big + shifted(jj)
        total = jnp.where(first, small, big)
        cnt = jnp.where(first, count(w_small), count(w_big))
        cols.append(total / cnt - shifted(0))
    pooled = jnp.concatenate(cols, axis=-1).astype(BF16)
    o_ref[0] = (_dot(pooled, w_ref[...]) * scale_ref[...]).astype(BF16)


def _pool_call(pc, w_blockdiag, scale):
    b, n, w = pc.shape
    return pl.pallas_call(
        _pool_kernel,
        grid=(b,),
        in_specs=[
            pl.BlockSpec((1, n, w), lambda bi: (bi, 0, 0)),
            _const_spec((w, w)),
            _const_spec((1, w)),
        ],
        out_specs=pl.BlockSpec((1, n, w), lambda bi: (bi, 0, 0)),
        out_shape=jax.ShapeDtypeStruct((b, n, w), BF16),
        scratch_shapes=[pltpu.VMEM((n + 2 * POOL_PAD, w), F32)],
        compiler_params=_cparams(1),
        name="pool",
    )(pc, w_blockdiag, scale)


def _merge_kernel(x_ref, mod_ref, g1_ref, ya_ref, yb_ref, yc_ref,
                  wg_ref, wa_ref, wb_ref, wc_ref, wo_ref, o_ref):
    d = x_ref.shape[-1]
    xt = x_ref[0]
    mod = mod_ref[0]
    u = _modulated_norm(xt, g1_ref[...], mod[:, :d], mod[:, d:2 * d]).astype(BF16)
    m = None
    for br, (y_ref, w_ref) in enumerate(((ya_ref, wa_ref), (yb_ref, wb_ref), (yc_ref, wc_ref))):
        gate = _sigmoid(_dot(u, wg_ref[:, br * d:(br + 1) * d]))
        term = gate * _dot(y_ref[0], w_ref[...])
        m = term if m is None else m + term
    o_ref[0] = xt + mod[:, 2 * d:3 * d] * _dot(m.astype(BF16), wo_ref[...])


def _merge_call(x, mod_l, mod_row_fn, g1, ya, yb, yc, wg, wa, wb, wc, wo, tm):
    b, n, d = x.shape

    def tok(width):
        return pl.BlockSpec((1, tm, width), lambda bi, i: (bi, i, 0))

    return pl.pallas_call(
        _merge_kernel,
        grid=(b, n // tm),
        in_specs=[
            tok(d),
            pl.BlockSpec((1, 1, 3 * d), lambda bi, i: (mod_row_fn(bi), 0, 0)),
            _const_spec((1, d)),
            tok(ya.shape[-1]), tok(yb.shape[-1]), tok(yc.shape[-1]),
            _const_spec(wg.shape), _const_spec(wa.shape), _const_spec(wb.shape),
            _const_spec(wc.shape), _const_spec(wo.shape),
        ],
        out_specs=tok(d),
        out_shape=jax.ShapeDtypeStruct((b, n, d), F32),
        compiler_params=_cparams(2),
        name="merge",
    )(x, mod_l, g1, ya, yb, yc, wg, wa, wb, wc, wo)


def _ffn_kernel(x_ref, mod_ref, g2_ref, w1_ref, w3_ref, w2_ref, o_ref):
    d = x_ref.shape[-1]
    xt = x_ref[0]
    mod = mod_ref[0]
    u = _modulated_norm(xt, g2_ref[...], mod[:, :d], mod[:, d:2 * d]).astype(BF16)
    a = _dot(u, w1_ref[...])
    hid = (a * _sigmoid(a)) * _dot(u, w3_ref[...])
    o_ref[0] = xt + mod[:, 2 * d:3 * d] * _dot(hid.astype(BF16), w2_ref[...])


def _ffn_call(x, mod_l, mod_row_fn, g2, w1, w3, w2, tm):
    b, n, d = x.shape
    tok = pl.BlockSpec((1, tm, d), lambda bi, i: (bi, i, 0))
    return pl.pallas_call(
        _ffn_kernel,
        grid=(b, n // tm),
        in_specs=[
            tok,
            pl.BlockSpec((1, 1, 3 * d), lambda bi, i: (mod_row_fn(bi), 0, 1)),
            _const_spec((1, d)),
            _const_spec(w1.shape), _const_spec(w3.shape), _const_spec(w2.shape),
        ],
        out_specs=tok,
        out_shape=jax.ShapeDtypeStruct((b, n, d), F32),
        compiler_params=_cparams(2),
        name="ffn",
    )(x, mod_l, g2, w1, w3, w2)


def _rope_tables(s):
    quarter = HEAD_DIM // 4
    t = jnp.arange(s)
    row = (t // GRID_W).astype(F32)
    col = (t % GRID_W).astype(F32)
    freqs = ROPE_THETA ** (-jnp.arange(quarter, dtype=F32) / quarter)
    ang_r = row[:, None] * freqs[None, :]
    ang_c = col[:, None] * freqs[None, :]
    zero = jnp.zeros_like(ang_r)
    cos = jnp.concatenate([jnp.cos(ang_r)] * 2 + [jnp.cos(ang_c)] * 2, axis=-1)
    sin_a = jnp.concatenate([-jnp.sin(ang_r), zero, -jnp.sin(ang_c), zero], axis=-1)
    sin_b = jnp.concatenate([zero, jnp.sin(ang_r), zero, jnp.sin(ang_c)], axis=-1)
    reps = MXU_DIM // HEAD_DIM
    return tuple(jnp.tile(a, (1, reps)) for a in (cos, sin_a, sin_b))


def _gqa_head_order():
    return [h for g in range(GQA_GROUP) for h in (g, GQA_GROUP + g)]


def _take_heads(a, order, axis):
    return jnp.concatenate([lax.slice_in_dim(a, h * HEAD_DIM, (h + 1) * HEAD_DIM, axis=axis) for h in order],
                           axis=axis)


def kernel(x, c, ctx, c_ctx, w_mod, b_mod, norm1_g, norm2_g, w_in, q_norm_a, k_norm_a, q_norm_b, k_norm_b,
           rpb_a, w_pool, pool_scale, w_br_a, w_br_b, w_br_c, w_out, w_ff1, w_ff3, w_ff2):
    b, s, d = x.shape
    lc = ctx.shape[1]
    depth = w_mod.shape[0]
    assert b + 1 <= MOD_ROWS and s == 2048 and s % (NA_Q_ROWS * GRID_W) == 0

    c_all = jnp.concatenate([c, c_ctx[None, :], jnp.zeros((MOD_ROWS - b - 1, d), F32)], axis=0)
    mod = _mod_call(c_all, w_mod, b_mod)

    rope_tabs = _rope_tables(s)
    gsum = jnp.kron(jnp.eye(MXU_DIM // HEAD_DIM, dtype=F32), jnp.ones((HEAD_DIM, HEAD_DIM), F32)).astype(BF16)
    head_order = _gqa_head_order()

    o_qa, o_ka, o_va = 0, NA_WIDTH, 2 * NA_WIDTH
    o_qb = 3 * NA_WIDTH
    o_kb = o_qb + GQA_WIDTH
    o_vb = o_kb + GQA_KV_WIDTH
    o_pc = o_vb + GQA_KV_WIDTH
    o_g = o_pc + POOL_WIDTH

    def proj_weight(w):
        return jnp.concatenate([
            w[:, o_qa:o_ka], w[:, o_ka:o_va], _take_heads(w[:, o_qb:o_kb], head_order, 1),
            w[:, o_kb:o_vb], w[:, o_va:o_qb], w[:, o_vb:o_pc], w[:, o_pc:o_g]], axis=1).astype(BF16)

    x_row = lambda bi: bi
    h_row = lambda bi: b
    tm_x, tm_h = 512, lc
    h = ctx
    for l in range(depth):
        last = l == depth - 1
        mod_l = mod[l].reshape(MOD_ROWS, 1, N_MOD * d)
        w_proj = proj_weight(w_in[l])
        w_gate = w_in[l][:, o_g:].astype(BF16)
        head_gain = jnp.concatenate([
            jnp.tile(q_norm_a[l], NA_HEADS) * ATTN_SCALE, jnp.tile(k_norm_a[l], NA_HEADS),
            jnp.tile(q_norm_b[l], GQA_Q_HEADS) * ATTN_SCALE, jnp.tile(k_norm_b[l], GQA_KV_HEADS)])[None, :]
        g1 = norm1_g[l][None, :]
        g2 = norm2_g[l][None, :]
        wa = w_br_a[l].astype(BF16)
        wb = _take_heads(w_br_b[l], head_order, 0).astype(BF16)
        wc = w_br_c[l].astype(BF16)
        wo = w_out[l].astype(BF16)
        w1 = w_ff1[l].astype(BF16)
        w3 = w_ff3[l].astype(BF16)
        w2 = w_ff2[l].astype(BF16)
        wp = jax.scipy.linalg.block_diag(*[w_pool[l, g] for g in range(len(POOL_WINDOWS))]).astype(BF16)
        pscale = pool_scale[l][None, :]
        bias = _na_bias_table(rpb_a[l])

        qa, ka, va, qb, kb, vb, pc = _proj_call(x, mod_l, x_row, g1, w_proj, gsum, head_gain, rope_tabs, tm_x)
        qa_h, ka_h, va_h, qb_h, kb_h, vb_h, pc_h = _proj_call(h, mod_l, h_row, g1, w_proj, gsum, head_gain,
                                                              None, tm_h)
        ya = _na_call(qa, ka, va, ka_h, va_h, bias)
        yb = _gqa_call(qb, kb, vb, kb_h, vb_h, 256)
        yc = _pool_call(pc, wp, pscale)
        x = _merge_call(x, mod_l, x_row, g1, ya, yb, yc, w_gate, wa, wb, wc, wo, tm_x)
        x = _ffn_call(x, mod_l, x_row, g2, w1, w3, w2, tm_x)
        if not last:
            ya_h, yb_h = _cattn_call(qa_h, ka_h, va_h, qb_h, kb_h, vb_h)
            yc_h = _pool_call(pc_h, wp, pscale)
            h = _merge_call(h, mod_l, h_row, g1, ya_h, yb_h, yc_h, w_gate, wa, wb, wc, wo, tm_h)
            h = _ffn_call(h, mod_l, h_row, g2, w1, w3, w2, tm_h)
    return x
```
